```python
import jax, jax.numpy as jnp
from jax import lax
import numpy as np

D_MODEL = 1024
BATCH = 8
SEQ = 2048
DEPTH = 4
DEC_BATCH = 32
DEC_SEQ = 8
PAST_LEN = 16384
PAGE_SIZE = 128

N_EVEN = (DEPTH + 1) // 2
N_ODD = DEPTH // 2
HG_HEADS = 4
HG_KDIM = 128
HG_VDIM = D_MODEL // 2 // HG_HEADS
HG_QDIM = HG_HEADS * HG_KDIM
HG_WIDTH = HG_HEADS * HG_VDIM
MLA_HEADS = 4
QK_NOPE = 128
QK_ROPE = 64
V_HEAD = 128
Q_LORA = 384
KV_LORA = 256
LAT_DIM = KV_LORA + QK_ROPE
MLA_WIDTH = MLA_HEADS * V_HEAD
MLA_SCALE = (QK_NOPE + QK_ROPE) ** -0.5
ROPE_THETA = 10000.0
Q_BLOCK = 128
NEG_BIG = -1e30
TINY = 1e-30
GLA_HEADS = 4
GLA_KEY = D_MODEL // 2
GLA_VAL = D_MODEL
GLA_HK = GLA_KEY // GLA_HEADS
GLA_HV = GLA_VAL // GLA_HEADS
GATE_RANK = 16
GATE_TAU = 16.0
D_FF = 4 * D_MODEL
CHUNK = 64
EPS = 1e-6

EVEN_SPLIT = [HG_QDIM, HG_QDIM, HG_WIDTH, HG_WIDTH, Q_LORA, KV_LORA, QK_ROPE]
EVEN_COLS = sum(EVEN_SPLIT)
ODD_SPLIT = [GLA_KEY, GLA_KEY, GLA_VAL, GLA_VAL, GATE_RANK]
ODD_COLS = sum(ODD_SPLIT)

kernel_name = 'hybrid_hgrn2_mla_gla_decode_step'


def _split(x, sizes):
    idx = np.cumsum(sizes)[:-1].tolist()
    return jnp.split(x, idx, axis=-1)


def rmsnorm(x, w):
    xf = x.astype(jnp.float32)
    y = xf * lax.rsqrt(jnp.mean(xf * xf, axis=-1, keepdims=True) + EPS)
    return (y * w.astype(jnp.float32)).astype(x.dtype)


def rope(x, pos):
    half = x.shape[-1] // 2
    inv = ROPE_THETA ** (-jnp.arange(half, dtype=jnp.float32) / half)
    ang = pos.astype(jnp.float32)[:, None] * inv[None, :]
    cos = jnp.cos(ang)[:, None, :]
    sin = jnp.sin(ang)[:, None, :]
    x1 = x[..., :half].astype(jnp.float32)
    x2 = x[..., half:].astype(jnp.float32)
    return jnp.concatenate([x1 * cos - x2 * sin, x1 * sin + x2 * cos], axis=-1).astype(x.dtype)


def gated_linear_scan(q, k, v, log_f, s0):
    B, T, H, K = q.shape
    V = v.shape[-1]
    L = min(CHUNK, T)
    n = -(-T // L)
    pad = n * L - T

    def blocks(a):
        a = jnp.pad(a.astype(jnp.float32), ((0, 0), (0, pad), (0, 0), (0, 0)))
        return a.reshape(B, n, L, H, a.shape[-1]).transpose(1, 0, 3, 2, 4)

    qc, kc, vc, gc = blocks(q), blocks(k), blocks(v), blocks(log_f)
    causal = jnp.tril(jnp.ones((L, L), dtype=bool))[:, :, None]

    def step(S, inp):
        qb, kb, vb, gb = inp
        b = jnp.cumsum(gb, axis=2)
        o_inter = jnp.einsum('bhtk,bhkv->bhtv', qb * jnp.exp(b), S)
        diff = b[:, :, :, None, :] - b[:, :, None, :, :]
        decay = jnp.where(causal, jnp.exp(jnp.minimum(diff, 0.0)), 0.0)
        scores = jnp.sum(qb[:, :, :, None, :] * decay * kb[:, :, None, :, :], axis=-1)
        o = o_inter + jnp.einsum('bhts,bhsv->bhtv', scores, vb)
        b_last = b[:, :, -1, :]
        k_dec = kb * jnp.exp(b_last[:, :, None, :] - b)
        S_new = jnp.exp(b_last)[..., None] * S + jnp.einsum('bhsk,bhsv->bhkv', k_dec, vb)
        return S_new, o

    S, o = lax.scan(step, s0.astype(jnp.float32), (qc, kc, vc, gc))
    o = o.transpose(1, 0, 3, 2, 4).reshape(B, n * L, H, V)[:, :T]
    return o.astype(v.dtype), S.astype(s0.dtype)


def hgrn2_mixer(q_raw, f_raw, i_raw, g_raw, lb, norm_w, s0):
    B, T, _ = q_raw.shape
    shp_k = (B, T, HG_HEADS, HG_KDIM)
    z = f_raw.reshape(shp_k).astype(jnp.float32)
    lbh = lb.reshape(HG_HEADS, HG_KDIM).astype(jnp.float32)
    f = lbh + (1.0 - lbh) * jax.nn.sigmoid(z)
    log_f = jnp.log(jnp.maximum(f, TINY))
    k = (1.0 - lbh) * jax.nn.sigmoid(-z)
    q = jax.nn.silu(q_raw.reshape(shp_k)) * (HG_KDIM ** -0.5)
    v = i_raw.reshape(B, T, HG_HEADS, HG_VDIM)
    o, s_new = gated_linear_scan(q, k, v, log_f, s0)
    o = rmsnorm(o, norm_w) * jax.nn.silu(g_raw.reshape(B, T, HG_HEADS, HG_VDIM))
    return o.reshape(B, T, HG_WIDTH), s_new


def mla_attend(q_lat, q_pe, lat, q_pos, k_pos):
    c_kv = lat[..., :KV_LORA]
    k_pe = lat[..., KV_LORA:]
    s = (jnp.einsum('bqhc,bkc->bhqk', q_lat, c_kv, preferred_element_type=jnp.float32)
         + jnp.einsum('bqhr,bkr->bhqk', q_pe, k_pe, preferred_element_type=jnp.float32)) * MLA_SCALE
    s = jnp.where(k_pos[None, :] <= q_pos[:, None], s, NEG_BIG)
    p = jax.nn.softmax(s, axis=-1)
    return jnp.einsum('bhqk,bkc->bqhc', p.astype(lat.dtype), c_kv)


def mla_mixer(c_q, c_kv_raw, k_r, pos, past_lat, q_norm_w, w_uq, kv_norm_w, w_uk, w_uv):
    B, T, _ = c_q.shape
    q = (rmsnorm(c_q, q_norm_w) @ w_uq).reshape(B, T, MLA_HEADS, QK_NOPE + QK_ROPE)
    q_nope = q[..., :QK_NOPE]
    q_pe = rope(q[..., QK_NOPE:], pos)
    c_kv = rmsnorm(c_kv_raw, kv_norm_w)
    k_pe = rope(k_r[:, :, None, :], pos)[:, :, 0, :]
    rows = jnp.concatenate([c_kv, k_pe], axis=-1)
    q_lat = jnp.einsum('bthn,hcn->bthc', q_nope, w_uk)
    if past_lat is None:
        blk = min(Q_BLOCK, T)
        nb = T // blk

        def blockwise(a):
            return a.reshape(B, nb, blk, *a.shape[2:]).swapaxes(0, 1)

        out = lax.map(lambda qs: mla_attend(qs[0], qs[1], rows, qs[2], pos),
                      (blockwise(q_lat), blockwise(q_pe), pos.reshape(nb, blk)))
        out = out.swapaxes(0, 1).reshape(B, T, MLA_HEADS, KV_LORA)
    else:
        lat = jnp.concatenate([past_lat, rows], axis=1)
        k_pos = jnp.arange(lat.shape[1])
        out = mla_attend(q_lat, q_pe, lat, pos, k_pos)
    o = jnp.einsum('bthc,hcv->bthv', out, w_uv)
    return o.reshape(B, T, MLA_WIDTH), rows


def even_mixer(h, pos, past_lat, s0, lb, w_in, hg_norm_w, q_norm_w, w_uq, kv_norm_w, w_uk, w_uv, w_out):
    hq, hf, hi, hg, cq, ckv, kr = _split(h @ w_in, EVEN_SPLIT)
    o_a, s_new = hgrn2_mixer(hq, hf, hi, hg, lb, hg_norm_w, s0)
    o_b, rows = mla_mixer(cq, ckv, kr, pos, past_lat, q_norm_w, w_uq, kv_norm_w, w_uk, w_uv)
    return jnp.concatenate([o_a, o_b], axis=-1) @ w_out, rows, s_new


def gla_mixer(h, s0, w_in, w_a2, b_a, norm_w, w_out):
    B, T, _ = h.shape
    q, k, v, r, a1 = _split(h @ w_in, ODD_SPLIT)
    log_a = jax.nn.log_sigmoid((a1 @ w_a2 + b_a).astype(jnp.float32)) / GATE_TAU
    shp_k = (B, T, GLA_HEADS, GLA_HK)
    o, s_new = gated_linear_scan(q.reshape(shp_k) * (GLA_HK ** -0.5), k.reshape(shp_k),
                                 v.reshape(B, T, GLA_HEADS, GLA_HV), log_a.reshape(shp_k), s0)
    o = rmsnorm(o, norm_w) * jax.nn.silu(r.reshape(B, T, GLA_HEADS, GLA_HV))
    return o.reshape(B, T, GLA_VAL) @ w_out, s_new


def sq_relu_mlp(h, w_up, w_down):
    a = jax.nn.relu(h @ w_up)
    return (a * a) @ w_down


def setup_inputs(seed: int = 0) -> dict:
    key = jax.random.key(seed)
    ks = jax.random.split(key, 25)
    f32 = jnp.float32

    def nrm(k, shape, scale):
        return jax.random.normal(k, shape, f32) * scale

    def gain(k, shape):
        return 1.0 + 0.02 * jax.random.normal(k, shape, f32)

    n_pages = PAST_LEN // PAGE_SIZE
    n_used = DEC_BATCH * n_pages
    n_phys = -(-n_used * 5 // 4)
    page_table = jax.random.permutation(ks[5], n_phys)[:n_used].reshape(DEC_BATCH, n_pages).astype(jnp.int32)
    return {
        'x_prompt': nrm(ks[0], (BATCH, SEQ, D_MODEL), 1.0),
        'x_sample': nrm(ks[1], (DEC_BATCH, DEC_SEQ, D_MODEL), 1.0),
        'cache_mla': nrm(ks[2], (N_EVEN, n_phys, PAGE_SIZE, LAT_DIM), 1.0),
        'state_hgrn': nrm(ks[3], (N_EVEN, DEC_BATCH, HG_HEADS, HG_KDIM, HG_VDIM), 0.5),
        'state_gla': nrm(ks[4], (N_ODD, DEC_BATCH, GLA_HEADS, GLA_HK, GLA_HV), 0.5),
        'page_table': page_table,
        'norm_mix': gain(ks[6], (DEPTH, D_MODEL)),
        'norm_mlp': gain(ks[7], (DEPTH, D_MODEL)),
        'norm_final': gain(ks[8], (D_MODEL,)),
        'w_in_even': nrm(ks[9], (N_EVEN, D_MODEL, EVEN_COLS), D_MODEL ** -0.5),
        'lower_bounds': nrm(ks[10], (N_EVEN, HG_QDIM), 0.5),
        'hgrn_norm': gain(ks[11], (N_EVEN, HG_VDIM)),
        'q_norm': gain(ks[12], (N_EVEN, Q_LORA)),
        'w_uq': nrm(ks[13], (N_EVEN, Q_LORA, MLA_HEADS * (QK_NOPE + QK_ROPE)), Q_LORA ** -0.5),
        'kv_norm': gain(ks[14], (N_EVEN, KV_LORA)),
        'w_uk': nrm(ks[15], (N_EVEN, MLA_HEADS, KV_LORA, QK_NOPE), QK_NOPE ** -0.5),
        'w_uv': nrm(ks[16], (N_EVEN, MLA_HEADS, KV_LORA, V_HEAD), KV_LORA ** -0.5),
        'w_out_even': nrm(ks[17], (N_EVEN, HG_WIDTH + MLA_WIDTH, D_MODEL), (HG_WIDTH + MLA_WIDTH) ** -0.5),
        'w_in_odd': nrm(ks[18], (N_ODD, D_MODEL, ODD_COLS), D_MODEL ** -0.5),
        'w_alpha2': nrm(ks[19], (N_ODD, GATE_RANK, GLA_KEY), GATE_RANK ** -0.5),
        'b_alpha': nrm(ks[20], (N_ODD, GLA_KEY), 0.01),
        'gla_norm': gain(ks[21], (N_ODD, GLA_HV)),
        'w_out_odd': nrm(ks[22], (N_ODD, GLA_VAL, D_MODEL), GLA_VAL ** -0.5),
        'w_up': nrm(ks[23], (DEPTH, D_MODEL, D_FF), D_MODEL ** -0.5),
        'w_down': nrm(ks[24], (DEPTH, D_FF, D_MODEL), D_FF ** -0.5),
    }


def reference(x_prompt, x_sample, cache_mla, state_hgrn, state_gla, page_table, norm_mix, norm_mlp,
              norm_final, w_in_even, lower_bounds, hgrn_norm, q_norm, w_uq, kv_norm, w_uk, w_uv,
              w_out_even, w_in_odd, w_alpha2, b_alpha, gla_norm, w_out_odd, w_up, w_down):
    B, T, _ = x_prompt.shape
    Bd, Td, _ = x_sample.shape
    n_pages = page_table.shape[1]
    past_len = n_pages * PAGE_SIZE
    pos_p = jnp.arange(T)
    pos_s = past_len + jnp.arange(Td)
    p_lb = jax.nn.softmax(lower_bounds.astype(jnp.float32), axis=0)
    lbs = jnp.cumsum(p_lb, axis=0) - p_lb[0]
    zero_hg = jnp.zeros((B, HG_HEADS, HG_KDIM, HG_VDIM), x_prompt.dtype)
    zero_gla = jnp.zeros((B, GLA_HEADS, GLA_HK, GLA_HV), x_prompt.dtype)

    xp, xs = x_prompt, x_sample
    mla_p, mla_s, hg_p, hg_s, gla_p, gla_s = [], [], [], [], [], []
    for layer in range(DEPTH):
        hp = rmsnorm(xp, norm_mix[layer])
        hs = rmsnorm(xs, norm_mix[layer])
        if layer % 2 == 0:
            e = layer // 2
            past = cache_mla[e][page_table].reshape(Bd, past_len, LAT_DIM)
            wts = (lbs[e], w_in_even[e], hgrn_norm[e], q_norm[e], w_uq[e], kv_norm[e],
                   w_uk[e], w_uv[e], w_out_even[e])
            dp, rows_p, sp = even_mixer(hp, pos_p, None, zero_hg, *wts)
            ds, rows_s, ss = even_mixer(hs, pos_s, past, state_hgrn[e], *wts)
            mla_p.append(rows_p)
            mla_s.append(rows_s)
            hg_p.append(sp)
            hg_s.append(ss)
        else:
            o = layer // 2
            wts = (w_in_odd[o], w_alpha2[o], b_alpha[o], gla_norm[o], w_out_odd[o])
            dp, sp = gla_mixer(hp, zero_gla, *wts)
            ds, ss = gla_mixer(hs, state_gla[o], *wts)
            gla_p.append(sp)
            gla_s.append(ss)
        xp = xp + dp
        xs = xs + ds
        xp = xp + sq_relu_mlp(rmsnorm(xp, norm_mlp[layer]), w_up[layer], w_down[layer])
        xs = xs + sq_relu_mlp(rmsnorm(xs, norm_mlp[layer]), w_up[layer], w_down[layer])

    y_prompt = rmsnorm(xp, norm_final)
    y_sample = rmsnorm(xs, norm_final)
    return (y_prompt, y_sample, jnp.stack(mla_p), jnp.stack(mla_s), jnp.stack(hg_p),
            jnp.stack(hg_s), jnp.stack(gla_p), jnp.stack(gla_s))
```

```python
import functools

import numpy as np
import jax
import jax.numpy as jnp
from jax import lax
from jax.experimental import pallas as pl
from jax.experimental.pallas import tpu as pltpu

F32 = jnp.float32
BF16 = jnp.bfloat16

D_MODEL = 1024
DEPTH = 4
PAGE_SIZE = 128
N_EVEN = (DEPTH + 1) // 2
HG_HEADS = 4
HG_KDIM = 128
HG_VDIM = 128
HG_QDIM = HG_HEADS * HG_KDIM
HG_WIDTH = HG_HEADS * HG_VDIM
MLA_HEADS = 4
QK_NOPE = 128
QK_ROPE = 64
V_HEAD = 128
Q_LORA = 384
KV_LORA = 256
LAT_DIM = KV_LORA + QK_ROPE
LAT_PAD = 384
MLA_SCALE = (QK_NOPE + QK_ROPE) ** -0.5
ROPE_THETA = 10000.0
NEG_BIG = -1e30
TINY = 1e-30
GLA_HEADS = 4
GLA_KEY = 512
GLA_VAL = 1024
GLA_HK = GLA_KEY // GLA_HEADS
GLA_HV = GLA_VAL // GLA_HEADS
GATE_RANK = 16
GATE_TAU = 16.0
D_FF = 4 * D_MODEL
CHUNK = 64
SUB = 16
EPS = 1e-6
LANES = 128

VMEM_LIMIT = 56 * 1024 * 1024

E_CQ, E_KR, E_CKV, E_Q, E_F, E_I, E_G = 0, 384, 512, 768, 1280, 1792, 2304
EVEN_PAD = 2816
O_Q, O_K, O_V, O_R, O_A = 0, 512, 1024, 2048, 3072
ODD_PAD = 3200


def _cparams(sem):
    return pltpu.CompilerParams(dimension_semantics=sem, vmem_limit_bytes=VMEM_LIMIT)


def _rms(x, w):
    return x * lax.rsqrt(jnp.mean(x * x, axis=-1, keepdims=True) + EPS) * w


def _dot(a, b):
    return jnp.dot(a, b, preferred_element_type=F32)


def _dot_nt(a, b):
    return lax.dot_general(a, b, (((1,), (1,)), ((), ())), preferred_element_type=F32)


def _dot_tn(a, b):
    return lax.dot_general(a, b, (((0,), (0,)), ((), ())), preferred_element_type=F32)


def _sigmoid(x):
    return 1.0 / (1.0 + jnp.exp(-x))


def _norm_proj_kernel(x_ref, nw_ref, w_ref, o_ref, *, col_chunk):
    h = _rms(x_ref[...], nw_ref[...]).astype(BF16)
    n_cols = o_ref.shape[1]
    for c0 in range(0, n_cols, col_chunk):
        c1 = min(c0 + col_chunk, n_cols)
        o_ref[:, c0:c1] = _dot(h, w_ref[:, c0:c1])


def norm_proj(x, norm_w, w, tm):
    n, d = x.shape
    c = w.shape[1]
    return pl.pallas_call(
        functools.partial(_norm_proj_kernel, col_chunk=512),
        out_shape=jax.ShapeDtypeStruct((n, c), F32),
        grid=(n // tm,),
        in_specs=[pl.BlockSpec((tm, d), lambda i: (i, 0)),
                  pl.BlockSpec((1, d), lambda i: (0, 0)),
                  pl.BlockSpec((d, c), lambda i: (0, 0))],
        out_specs=pl.BlockSpec((tm, c), lambda i: (i, 0)),
        compiler_params=_cparams(("parallel",)),
        name="norm_proj",
    )(x, norm_w.reshape(1, d), w)


def _out_mlp_kernel(*refs, n_parts, final_norm):
    x_ref = refs[0]
    o_refs = refs[1:1 + n_parts]
    wo_ref, nw_ref, wup_ref, wdn_ref, nf_ref, y_ref, xm_sc, h_sc, acc_sc = refs[1 + n_parts:]
    j = pl.program_id(1)

    @pl.when(j == 0)
    def _():
        xm = x_ref[...]
        r0 = 0
        for o_ref in o_refs:
            w = o_ref.shape[1]
            xm = xm + _dot(o_ref[...].astype(BF16), wo_ref[r0:r0 + w, :])
            r0 += w
        xm_sc[...] = xm
        h_sc[...] = _rms(xm, nw_ref[...]).astype(BF16)
        acc_sc[...] = jnp.zeros_like(acc_sc)

    a = jnp.maximum(_dot(h_sc[...], wup_ref[...]), 0.0)
    acc_sc[...] += _dot((a * a).astype(BF16), wdn_ref[...])

    @pl.when(j == pl.num_programs(1) - 1)
    def _():
        y = xm_sc[...] + acc_sc[...]
        if final_norm:
            y = _rms(y, nf_ref[...])
        y_ref[...] = y


def out_mlp(x, o_parts, w_out, norm_w, w_up, w_down, norm_final, final_norm, tm, tf):
    n, d = x.shape
    dff = w_up.shape[1]
    n_parts = len(o_parts)
    in_specs = [pl.BlockSpec((tm, d), lambda i, j: (i, 0))]
    in_specs += [pl.BlockSpec((tm, o.shape[1]), lambda i, j: (i, 0)) for o in o_parts]
    in_specs += [pl.BlockSpec((d, d), lambda i, j: (0, 0)),
                 pl.BlockSpec((1, d), lambda i, j: (0, 0)),
                 pl.BlockSpec((d, tf), lambda i, j: (0, j)),
                 pl.BlockSpec((tf, d), lambda i, j: (j, 0)),
                 pl.BlockSpec((1, d), lambda i, j: (0, 0))]
    return pl.pallas_call(
        functools.partial(_out_mlp_kernel, n_parts=n_parts, final_norm=final_norm),
        out_shape=jax.ShapeDtypeStruct((n, d), F32),
        grid=(n // tm, dff // tf),
        in_specs=in_specs,
        out_specs=pl.BlockSpec((tm, d), lambda i, j: (i, 0)),
        scratch_shapes=[pltpu.VMEM((tm, d), F32), pltpu.VMEM((tm, d), BF16), pltpu.VMEM((tm, d), F32)],
        compiler_params=_cparams(("parallel", "arbitrary")),
        name="out_mlp",
    )(x, *o_parts, w_out, norm_w.reshape(1, d), w_up, w_down, norm_final.reshape(1, d))


def _pad_rows(a, rows):
    if a.shape[0] == rows:
        return a
    return jnp.concatenate([a, jnp.zeros((rows - a.shape[0], a.shape[1]), a.dtype)], axis=0)


def _chunk_core(q, k, v, g, st_ref, b_sc, k_sc):
    L, K = q.shape
    row = lax.broadcasted_iota(jnp.int32, (L, LANES), 0)
    col = lax.broadcasted_iota(jnp.int32, (L, LANES), 1)
    tri = jnp.where(col <= row, 1.0, 0.0).astype(BF16)
    g_hi = g.astype(BF16)
    r1 = g - g_hi.astype(F32)
    g_mid = r1.astype(BF16)
    g_lo = (r1 - g_mid.astype(F32)).astype(BF16)
    b = (_dot(tri, _pad_rows(g_hi, LANES)) + _dot(tri, _pad_rows(g_mid, LANES))
         + _dot(tri, _pad_rows(g_lo, LANES)))
    b_sc[...] = b
    k_sc[...] = k
    b_last = b[L - 1:L, :]
    st = st_ref[...]
    o = _dot_nt((q * jnp.exp(b)).astype(BF16), st.astype(BF16))

    lane = lax.broadcasted_iota(jnp.int32, (SUB, LANES), 1)
    srow = lax.broadcasted_iota(jnp.int32, (SUB, LANES), 0)
    blocks = []
    for i in range(L // SUB):
        qi = q[SUB * i:SUB * (i + 1), :]
        bi = b[SUB * i:SUB * (i + 1), :]
        a_i = jnp.zeros((SUB, LANES), F32)
        for sl in range(SUB):
            s = SUB * i + sl
            bs = b_sc[s:s + 1, :]
            ks = k_sc[s:s + 1, :]
            dec = jnp.exp(jnp.minimum(bi - bs, 0.0))
            c = jnp.sum(qi * dec * ks, axis=-1, keepdims=True)
            a_i = jnp.where(lane == s, c, a_i)
        if i > 0:
            r = b[SUB * i - 1:SUB * i, :]
            qq = qi * jnp.exp(jnp.minimum(bi - r, 0.0))
            kk = k * jnp.exp(jnp.minimum(r - b, 0.0))
            off = _dot_nt(qq.astype(BF16), _pad_rows(kk.astype(BF16), LANES))
            a_i = jnp.where(lane < SUB * i, off, a_i)
        a_i = jnp.where(lane <= srow + SUB * i, a_i, 0.0)
        blocks.append(a_i)
    a = blocks[0] if len(blocks) == 1 else jnp.concatenate(blocks, axis=0)
    v_pad = _pad_rows(v.astype(BF16), LANES)
    o = o + _dot(a.astype(BF16), v_pad)
    k_dec = k * jnp.exp(b_last - b)
    st_ref[...] = st * jnp.exp(b_last) + _dot_tn(v_pad, _pad_rows(k_dec.astype(BF16), LANES))
    return o


def _scan_body(gates_fn, post_fn, s0_ref, o_ref, sout_ref, st_sc, b_sc, k_sc, *, rows, L):
    t = pl.program_id(2)

    @pl.when(t == 0)
    def _():
        st_sc[...] = s0_ref[...].T

    n_chunks = o_ref.shape[0] // rows

    def chunk(r0):
        q, k, v, g, aux = gates_fn(r0)
        o = _chunk_core(_pad_rows(q, L), _pad_rows(k, L), _pad_rows(v, L), _pad_rows(g, L),
                        st_sc, b_sc, k_sc)
        o_ref[pl.ds(r0, rows), :] = post_fn(o[:rows], aux).astype(o_ref.dtype)

    def loop_body(c, carry):
        chunk(pl.multiple_of(c * rows, rows))
        return carry

    if n_chunks == 1:
        chunk(0)
    else:
        lax.fori_loop(0, n_chunks, loop_body, 0)

    @pl.when(t == pl.num_programs(2) - 1)
    def _():
        sout_ref[...] = st_sc[...].T


def _hgrn_kernel(q_ref, f_ref, i_ref, g_ref, lb_ref, nw_ref, s0_ref, o_ref, sout_ref,
                 st_sc, b_sc, k_sc, *, layer, rows, L):
    lbr = [lb_ref[i:i + 1, :] for i in range(lb_ref.shape[0])]
    mx = functools.reduce(jnp.maximum, lbr)
    ex = [jnp.exp(r - mx) for r in lbr]
    den = functools.reduce(lambda a, c: a + c, ex)
    p = [e / den for e in ex]
    cum = p[0]
    for i in range(1, layer + 1):
        cum = cum + p[i]
    lb = cum - p[0]

    def gates(r0):
        z = f_ref[pl.ds(r0, rows), :]
        qr = q_ref[pl.ds(r0, rows), :]
        f = lb + (1.0 - lb) * _sigmoid(z)
        g = jnp.log(jnp.maximum(f, TINY))
        k = (1.0 - lb) * _sigmoid(-z)
        q = qr * _sigmoid(qr) * (HG_KDIM ** -0.5)
        return q, k, i_ref[pl.ds(r0, rows), :], g, g_ref[pl.ds(r0, rows), :]

    def post(o, gr):
        return _rms(o, nw_ref[...]) * (gr * _sigmoid(gr))

    _scan_body(gates, post, s0_ref, o_ref, sout_ref, st_sc, b_sc, k_sc, rows=rows, L=L)


def _gla_kernel(q_ref, k_ref, v_ref, r_ref, a_ref, wa_ref, ba_ref, nw_ref, s0_ref, o_ref, sout_ref,
                st_sc, b_sc, k_sc, *, rows, L):
    def gates(r0):
        a1 = _pad_rows(a_ref[pl.ds(r0, rows), :], L).astype(BF16)
        x = (_dot(a1, wa_ref[...]) + ba_ref[...])[:rows]
        g = (jnp.minimum(x, 0.0) - jnp.log(1.0 + jnp.exp(-jnp.abs(x)))) / GATE_TAU
        q = q_ref[pl.ds(r0, rows), :] * (GLA_HK ** -0.5)
        return q, k_ref[pl.ds(r0, rows), :], v_ref[pl.ds(r0, rows), :], g, r_ref[pl.ds(r0, rows), :]

    def post(o, rr):
        return _rms(o, nw_ref[...]) * (rr * _sigmoid(rr))

    _scan_body(gates, post, s0_ref, o_ref, sout_ref, st_sc, b_sc, k_sc, rows=rows, L=L)


def _scan_geometry(T):
    rows = min(CHUNK, T)
    L = max(rows, SUB)
    tc = min(T, 512)
    return tc, rows, L


def hgrn_scan(P, lower_bounds, norm_w, s0_all, layer, B, T, out_dtype):
    tc, rows, L = _scan_geometry(T)
    nt = T // tc
    K, V, H = HG_KDIM, HG_VDIM, HG_HEADS
    if s0_all is None:
        s0_all = jnp.zeros((1, B, H, K, V), F32)
        e_s0 = 0
    else:
        e_s0 = layer

    def colspec(off):
        return pl.BlockSpec((tc, LANES), lambda b, h, t: (b * nt + t, off // LANES + h))

    return pl.pallas_call(
        functools.partial(_hgrn_kernel, layer=layer, rows=rows, L=L),
        out_shape=(jax.ShapeDtypeStruct((B * T, H * V), out_dtype),
                   jax.ShapeDtypeStruct((B, H, K, V), F32)),
        grid=(B, H, nt),
        in_specs=[colspec(E_Q), colspec(E_F), colspec(E_I), colspec(E_G),
                  pl.BlockSpec((lower_bounds.shape[0], LANES), lambda b, h, t: (0, h)),
                  pl.BlockSpec((1, V), lambda b, h, t: (0, 0)),
                  pl.BlockSpec((None, None, None, K, V), lambda b, h, t: (e_s0, b, h, 0, 0))],
        out_specs=(pl.BlockSpec((tc, V), lambda b, h, t: (b * nt + t, h)),
                   pl.BlockSpec((None, None, K, V), lambda b, h, t: (b, h, 0, 0))),
        scratch_shapes=[pltpu.VMEM((V, K), F32), pltpu.VMEM((L, K), F32), pltpu.VMEM((L, K), F32)],
        compiler_params=_cparams(("parallel", "parallel", "arbitrary")),
        name="hgrn_scan",
    )(P, P, P, P, lower_bounds, norm_w.reshape(1, V), s0_all)


def gla_scan(P, w_a2p, b_a, norm_w, s0_all, layer, B, T, out_dtype):
    tc, rows, L = _scan_geometry(T)
    nt = T // tc
    K, V, H = GLA_HK, GLA_HV, GLA_HEADS
    if s0_all is None:
        s0_all = jnp.zeros((1, B, H, K, V), F32)
        e_s0 = 0
    else:
        e_s0 = layer

    def colspec(off, w):
        return pl.BlockSpec((tc, w), lambda b, h, t: (b * nt + t, off // w + h))

    return pl.pallas_call(
        functools.partial(_gla_kernel, rows=rows, L=L),
        out_shape=(jax.ShapeDtypeStruct((B * T, H * V), out_dtype),
                   jax.ShapeDtypeStruct((B, H, K, V), F32)),
        grid=(B, H, nt),
        in_specs=[colspec(O_Q, K), colspec(O_K, K), colspec(O_V, V), colspec(O_R, V),
                  pl.BlockSpec((tc, LANES), lambda b, h, t: (b * nt + t, O_A // LANES)),
                  pl.BlockSpec((LANES, K), lambda b, h, t: (0, h)),
                  pl.BlockSpec((1, K), lambda b, h, t: (0, h)),
                  pl.BlockSpec((1, V), lambda b, h, t: (0, 0)),
                  pl.BlockSpec((None, None, None, K, V), lambda b, h, t: (e_s0, b, h, 0, 0))],
        out_specs=(pl.BlockSpec((tc, V), lambda b, h, t: (b * nt + t, h)),
                   pl.BlockSpec((None, None, K, V), lambda b, h, t: (b, h, 0, 0))),
        scratch_shapes=[pltpu.VMEM((V, K), F32), pltpu.VMEM((L, K), F32), pltpu.VMEM((L, K), F32)],
        compiler_params=_cparams(("parallel", "parallel", "arbitrary")),
        name="gla_scan",
    )(P, P, P, P, P, w_a2p, b_a.reshape(1, GLA_KEY), norm_w.reshape(1, V), s0_all)


def _mla_prep_kernel(cq_ref, kr_ref, ckv_ref, qnw_ref, kvnw_ref, wn_ref, wpe_ref, wper_ref, wuk_ref,
                     ta_ref, tb_ref, tc_ref, rows_ref, kvp_ref, qc_ref):
    cqn = _rms(cq_ref[...], qnw_ref[...]).astype(BF16)
    qn = _dot(cqn, wn_ref[...])
    qpe = _dot(cqn, wpe_ref[...])
    qper = _dot(cqn, wper_ref[...])
    ta = ta_ref[...]
    tb = tb_ref[...]
    for h in range(MLA_HEADS):
        sl = slice(h * LANES, (h + 1) * LANES)
        q_lat = _dot(qn[:, sl].astype(BF16), wuk_ref[h])
        q_pe = qpe[:, sl] * ta + qper[:, sl] * tb
        qc_ref[:, h * LAT_PAD:h * LAT_PAD + KV_LORA] = q_lat.astype(qc_ref.dtype)
        qc_ref[:, h * LAT_PAD + KV_LORA:(h + 1) * LAT_PAD] = q_pe.astype(qc_ref.dtype)
    ckvn = _rms(ckv_ref[...], kvnw_ref[...])
    pk = kr_ref[...] * tc_ref[...]
    kf = pk + pltpu.roll(pk, QK_ROPE, axis=1)
    lane = lax.broadcasted_iota(jnp.int32, kf.shape, 1)
    kpe = jnp.where(lane < QK_ROPE, kf, 0.0)
    rows_ref[:, :KV_LORA] = ckvn
    rows_ref[:, KV_LORA:] = kf[:, :QK_ROPE]
    kvp_ref[:, :KV_LORA] = ckvn.astype(kvp_ref.dtype)
    kvp_ref[:, KV_LORA:] = kpe.astype(kvp_ref.dtype)


def mla_prep(P, qn_w, kvn_w, wn, wpe, wper, wukT, tabs, tab_blocks, tm, mid_dtype):
    n = P.shape[0]
    ta, tb, tcc = tabs

    def tabspec():
        return pl.BlockSpec((tm, LANES), lambda i: (i % tab_blocks, 0))

    def full(a):
        nd = a.ndim
        return pl.BlockSpec(a.shape, lambda i: (0,) * nd)

    qn_w = qn_w.reshape(1, Q_LORA)
    kvn_w = kvn_w.reshape(1, KV_LORA)
    return pl.pallas_call(
        _mla_prep_kernel,
        out_shape=(jax.ShapeDtypeStruct((n, LAT_DIM), F32),
                   jax.ShapeDtypeStruct((n, LAT_PAD), mid_dtype),
                   jax.ShapeDtypeStruct((n, MLA_HEADS * LAT_PAD), mid_dtype)),
        grid=(n // tm,),
        in_specs=[pl.BlockSpec((tm, Q_LORA), lambda i: (i, E_CQ // Q_LORA)),
                  pl.BlockSpec((tm, LANES), lambda i: (i, E_KR // LANES)),
                  pl.BlockSpec((tm, KV_LORA), lambda i: (i, E_CKV // KV_LORA)),
                  full(qn_w), full(kvn_w), full(wn), full(wpe), full(wper), full(wukT),
                  tabspec(), tabspec(), tabspec()],
        out_specs=(pl.BlockSpec((tm, LAT_DIM), lambda i: (i, 0)),
                   pl.BlockSpec((tm, LAT_PAD), lambda i: (i, 0)),
                   pl.BlockSpec((tm, MLA_HEADS * LAT_PAD), lambda i: (i, 0))),
        compiler_params=_cparams(("parallel",)),
        name="mla_prep",
    )(P, P, P, qn_w, kvn_w, wn, wpe, wper, wukT, ta, tb, tcc)


def _softmax_step(s, v, m_sc, l_sc, acc_sc):
    m_prev = m_sc[...]
    m_new = jnp.maximum(m_prev, jnp.max(s, axis=-1, keepdims=True))
    alpha = jnp.exp(m_prev - m_new)
    p = jnp.exp(s - m_new)
    l_sc[...] = alpha * l_sc[...] + jnp.sum(p, axis=-1, keepdims=True)
    acc_sc[...] = alpha * acc_sc[...] + _dot(p.astype(BF16), v)
    m_sc[...] = m_new


def _flash_kernel(q_ref, kv_ref, wuv_ref, o_ref, qs_sc, m_sc, l_sc, acc_sc, *, tq):
    qi = pl.program_id(1)
    H = MLA_HEADS
    for h in range(H):
        qs_sc[h * tq:(h + 1) * tq, :] = q_ref[:, h * LAT_PAD:(h + 1) * LAT_PAD]
    m_sc[...] = jnp.full_like(m_sc, NEG_BIG)
    l_sc[...] = jnp.zeros_like(l_sc)
    acc_sc[...] = jnp.zeros_like(acc_sc)

    def block(kj, masked):
        k0 = pl.multiple_of(kj * tq, tq)
        kv = kv_ref[pl.ds(k0, tq), :]
        s = _dot_nt(qs_sc[...], kv) * MLA_SCALE
        if masked:
            row = lax.broadcasted_iota(jnp.int32, s.shape, 0) & (tq - 1)
            col = lax.broadcasted_iota(jnp.int32, s.shape, 1)
            s = jnp.where(col <= row, s, NEG_BIG)
        _softmax_step(s, kv[:, :KV_LORA], m_sc, l_sc, acc_sc)

    def body(kj, carry):
        block(kj, False)
        return carry

    lax.fori_loop(0, qi, body, 0)
    block(qi, True)
    for h in range(H):
        out = (acc_sc[h * tq:(h + 1) * tq, :] / l_sc[h * tq:(h + 1) * tq, :]).astype(BF16)
        o_ref[:, h * V_HEAD:(h + 1) * V_HEAD] = _dot(out, wuv_ref[h]).astype(o_ref.dtype)


def flash_prompt(qc, kvp, wuv, B, T, tq):
    nq = T // tq
    H = MLA_HEADS
    return pl.pallas_call(
        functools.partial(_flash_kernel, tq=tq),
        out_shape=jax.ShapeDtypeStruct((B * T, H * V_HEAD), BF16),
        grid=(B, nq),
        in_specs=[pl.BlockSpec((tq, H * LAT_PAD), lambda b, i: (b * nq + i, 0)),
                  pl.BlockSpec((T, LAT_PAD), lambda b, i: (b, 0)),
                  pl.BlockSpec(wuv.shape, lambda b, i: (0, 0, 0))],
        out_specs=pl.BlockSpec((tq, H * V_HEAD), lambda b, i: (b * nq + i, 0)),
        scratch_shapes=[pltpu.VMEM((H * tq, LAT_PAD), BF16), pltpu.VMEM((H * tq, 1), F32),
                        pltpu.VMEM((H * tq, 1), F32), pltpu.VMEM((H * tq, KV_LORA), F32)],
        compiler_params=_cparams(("parallel", "arbitrary")),
        name="flash_prompt",
    )(qc, kvp, wuv)


def _paged_kernel(pt_ref, *refs, n_pages, td):
    page_refs = refs[:n_pages]
    q_ref, new_ref, wuv_ref, o_ref, qs_sc, kb_sc, kn_sc, m_sc, l_sc, acc_sc = refs[n_pages:]
    c = pl.program_id(1)
    H = MLA_HEADS

    @pl.when(c == 0)
    def _():
        for h in range(H):
            qs_sc[h * td:(h + 1) * td, :] = q_ref[:, h * LAT_PAD:(h + 1) * LAT_PAD]
        kb_sc[:, KV_LORA:] = jnp.zeros((kb_sc.shape[0], LAT_PAD - KV_LORA), BF16)
        m_sc[...] = jnp.full_like(m_sc, NEG_BIG)
        l_sc[...] = jnp.zeros_like(l_sc)
        acc_sc[...] = jnp.zeros_like(acc_sc)

    for j in range(n_pages):
        kb_sc[j * PAGE_SIZE:(j + 1) * PAGE_SIZE, :LAT_DIM] = page_refs[j][...].astype(BF16)
    qs = qs_sc[...].astype(BF16)
    kb = kb_sc[...]
    s = _dot_nt(qs, kb) * MLA_SCALE
    _softmax_step(s, kb[:, :KV_LORA], m_sc, l_sc, acc_sc)

    @pl.when(c == pl.num_programs(1) - 1)
    def _():
        kn_sc[...] = jnp.zeros_like(kn_sc)
        kn_sc[:td, :] = new_ref[...]
        kn = kn_sc[...].astype(BF16)
        sn = _dot_nt(qs, kn) * MLA_SCALE
        row = lax.broadcasted_iota(jnp.int32, sn.shape, 0) & (td - 1)
        col = lax.broadcasted_iota(jnp.int32, sn.shape, 1)
        sn = jnp.where(col <= row, sn, NEG_BIG)
        _softmax_step(sn, kn[:, :KV_LORA], m_sc, l_sc, acc_sc)
        out = (acc_sc[...] / l_sc[...]).astype(BF16)
        for h in range(H):
            o_ref[:, h * V_HEAD:(h + 1) * V_HEAD] = _dot(out, wuv_ref[h])[h * td:(h + 1) * td, :]


def paged_sample(page_table, cache, layer, qc, kvp, wuv, Bd, Td, pages_per_step):
    n_log = page_table.shape[1]
    H = MLA_HEADS
    npp = pages_per_step
    pt = page_table.reshape(-1)

    def page_spec(j):
        return pl.BlockSpec((None, None, PAGE_SIZE, LAT_DIM),
                            lambda b, c, pt_ref: (layer, pt_ref[b * n_log + c * npp + j], 0, 0))

    grid_spec = pltpu.PrefetchScalarGridSpec(
        num_scalar_prefetch=1,
        grid=(Bd, n_log // npp),
        in_specs=[page_spec(j) for j in range(npp)] + [
            pl.BlockSpec((Td, H * LAT_PAD), lambda b, c, pt_ref: (b, 0)),
            pl.BlockSpec((Td, LAT_PAD), lambda b, c, pt_ref: (b, 0)),
            pl.BlockSpec(wuv.shape, lambda b, c, pt_ref: (0, 0, 0))],
        out_specs=pl.BlockSpec((Td, H * V_HEAD), lambda b, c, pt_ref: (b, 0)),
        scratch_shapes=[pltpu.VMEM((H * Td, LAT_PAD), F32),
                        pltpu.VMEM((npp * PAGE_SIZE, LAT_PAD), BF16),
                        pltpu.VMEM((LANES, LAT_PAD), F32),
                        pltpu.VMEM((H * Td, 1), F32), pltpu.VMEM((H * Td, 1), F32),
                        pltpu.VMEM((H * Td, KV_LORA), F32)])
    return pl.pallas_call(
        functools.partial(_paged_kernel, n_pages=npp, td=Td),
        out_shape=jax.ShapeDtypeStruct((Bd * Td, H * V_HEAD), F32),
        grid_spec=grid_spec,
        compiler_params=_cparams(("parallel", "arbitrary")),
        name="paged_sample",
    )(pt, *([cache] * npp), qc, kvp, wuv)


def _rope_tables(pos):
    half = QK_ROPE // 2
    inv = ROPE_THETA ** (-jnp.arange(half, dtype=F32) / half)
    ang = pos.astype(F32)[:, None] * inv[None, :]
    cos2 = jnp.concatenate([jnp.cos(ang)] * 2, axis=1)
    sin_s = jnp.concatenate([-jnp.sin(ang), jnp.sin(ang)], axis=1)
    return (jnp.concatenate([cos2, cos2], axis=1), jnp.concatenate([sin_s, sin_s], axis=1),
            jnp.concatenate([cos2, sin_s], axis=1))


def _swap_halves(w):
    half = QK_ROPE // 2
    shp = w.shape
    w = w.reshape(shp[:-1] + (shp[-1] // QK_ROPE, 2, half))
    return w[..., ::-1, :].reshape(shp)


def _even_weights(w_in, w_uq, w_uk, w_uv):
    hq, hf, hi, hg, cq, ckv, kr = jnp.split(
        w_in, np.cumsum([HG_QDIM, HG_QDIM, HG_WIDTH, HG_WIDTH, Q_LORA, KV_LORA]).tolist(), axis=1)
    w_in_p = jnp.concatenate([cq, kr, _swap_halves(kr), ckv, hq, hf, hi, hg], axis=1).astype(BF16)
    wq = w_uq.reshape(Q_LORA, MLA_HEADS, QK_NOPE + QK_ROPE)
    wn = wq[:, :, :QK_NOPE].reshape(Q_LORA, MLA_HEADS * QK_NOPE).astype(BF16)
    wpe = wq[:, :, QK_NOPE:]
    zpad = jnp.zeros_like(wpe)
    wpe_p = jnp.concatenate([wpe, zpad], axis=2).reshape(Q_LORA, MLA_HEADS * LANES).astype(BF16)
    wper_p = jnp.concatenate([_swap_halves(wpe), zpad], axis=2).reshape(Q_LORA, MLA_HEADS * LANES).astype(BF16)
    wukT = jnp.swapaxes(w_uk, 1, 2).astype(BF16)
    return w_in_p, wn, wpe_p, wper_p, wukT, w_uv.astype(BF16)


def _odd_weights(w_in, w_a2):
    w_in_p = jnp.pad(w_in, ((0, 0), (0, ODD_PAD - w_in.shape[1]))).astype(BF16)
    w_a2p = jnp.pad(w_a2, ((0, LANES - GATE_RANK), (0, 0))).astype(BF16)
    return w_in_p, w_a2p


def kernel(x_prompt, x_sample, cache_mla, state_hgrn, state_gla, page_table, norm_mix, norm_mlp, norm_final, w_in_even, lower_bounds, hgrn_norm, q_norm, w_uq, kv_norm, w_uk, w_uv, w_out_even, w_in_odd, w_alpha2, b_alpha, gla_norm, w_out_odd, w_up, w_down):
    B, T, D = x_prompt.shape
    Bd, Td, _ = x_sample.shape
    past_len = page_table.shape[1] * PAGE_SIZE
    xp = x_prompt.reshape(B * T, D)
    xs = x_sample.reshape(Bd * Td, D)
    tm_p, tm_s = 512, Bd * Td
    tabs_p = _rope_tables(jnp.arange(T))
    tabs_s = tuple(jnp.tile(t, (Bd, 1)) for t in _rope_tables(past_len + jnp.arange(Td)))
    tm_mla = 256
    w_up_b = w_up.astype(BF16)
    w_down_b = w_down.astype(BF16)

    mla_p, mla_s, hg_p, hg_s, gla_p, gla_s = [], [], [], [], [], []
    for layer in range(DEPTH):
        last = layer == DEPTH - 1
        if layer % 2 == 0:
            e = layer // 2
            w_in_p, wn, wpe_p, wper_p, wukT, wuv = _even_weights(w_in_even[e], w_uq[e], w_uk[e], w_uv[e])
            w_out = w_out_even[e].astype(BF16)
            Pp = norm_proj(xp, norm_mix[layer], w_in_p, tm_p)
            Ps = norm_proj(xs, norm_mix[layer], w_in_p, tm_s)
            oa_p, sp = hgrn_scan(Pp, lower_bounds, hgrn_norm[e], None, e, B, T, BF16)
            oa_s, ss = hgrn_scan(Ps, lower_bounds, hgrn_norm[e], state_hgrn, e, Bd, Td, F32)
            rows_p, kvp_p, qc_p = mla_prep(Pp, q_norm[e], kv_norm[e], wn, wpe_p, wper_p, wukT,
                                           tabs_p, T // tm_mla, tm_mla, BF16)
            rows_s, kvp_s, qc_s = mla_prep(Ps, q_norm[e], kv_norm[e], wn, wpe_p, wper_p, wukT,
                                           tabs_s, 1, Bd * Td, F32)
            ob_p = flash_prompt(qc_p, kvp_p, wuv, B, T, 256)
            ob_s = paged_sample(page_table, cache_mla, e, qc_s, kvp_s, wuv, Bd, Td, 16)
            mla_p.append(rows_p.reshape(B, T, LAT_DIM))
            mla_s.append(rows_s.reshape(Bd, Td, LAT_DIM))
            hg_p.append(sp)
            hg_s.append(ss)
            parts_p, parts_s = [oa_p, ob_p], [oa_s, ob_s]
        else:
            o = layer // 2
            w_in_p, w_a2p = _odd_weights(w_in_odd[o], w_alpha2[o])
            w_out = w_out_odd[o].astype(BF16)
            Pp = norm_proj(xp, norm_mix[layer], w_in_p, tm_p)
            Ps = norm_proj(xs, norm_mix[layer], w_in_p, tm_s)
            o_p, sp = gla_scan(Pp, w_a2p, b_alpha[o], gla_norm[o], None, o, B, T, BF16)
            o_s, ss = gla_scan(Ps, w_a2p, b_alpha[o], gla_norm[o], state_gla, o, Bd, Td, F32)
            gla_p.append(sp)
            gla_s.append(ss)
            parts_p, parts_s = [o_p], [o_s]
        xp = out_mlp(xp, parts_p, w_out, norm_mlp[layer], w_up_b[layer], w_down_b[layer],
                     norm_final, last, tm_p, 1024)
        xs = out_mlp(xs, parts_s, w_out, norm_mlp[layer], w_up_b[layer], w_down_b[layer],
                     norm_final, last, tm_s, 1024)

    return (xp.reshape(B, T, D), xs.reshape(Bd, Td, D), jnp.stack(mla_p), jnp.stack(mla_s),
            jnp.stack(hg_p), jnp.stack(hg_s), jnp.stack(gla_p), jnp.stack(gla_s))
```

```python
import functools

import numpy as np
import jax
import jax.numpy as jnp
from jax import lax
from jax.experimental import pallas as pl
from jax.experimental.pallas import tpu as pltpu

F32 = jnp.float32
BF16 = jnp.bfloat16

D_MODEL = 1024
DEPTH = 4
PAGE_SIZE = 128
HG_HEADS = 4
HG_KDIM = 128
HG_VDIM = 128
HG_QDIM = HG_HEADS * HG_KDIM
HG_WIDTH = HG_HEADS * HG_VDIM
MLA_HEADS = 4
QK_NOPE = 128
QK_ROPE = 64
V_HEAD = 128
Q_LORA = 384
KV_LORA = 256
LAT_DIM = KV_LORA + QK_ROPE
LAT_PAD = 384
MLA_SCALE = (QK_NOPE + QK_ROPE) ** -0.5
LOG2E = 1.4426950408889634
ROPE_THETA = 10000.0
NEG_BIG = -1e30
TINY = 1e-30
GLA_HEADS = 4
GLA_KEY = 512
GLA_VAL = 1024
GLA_HK = GLA_KEY // GLA_HEADS
GLA_HV = GLA_VAL // GLA_HEADS
GATE_RANK = 16
GATE_TAU = 16.0
CHUNK = 64
SUB = 16
EPS = 1e-6
LANES = 128

VMEM_LIMIT = 56 * 1024 * 1024

E_CQ, E_KR, E_Q, E_F, E_I, E_G, E_CKV = 0, 384, 512, 1024, 1536, 2048, 2560
EVEN_PAD = 2816
O_Q, O_K, O_V, O_R, O_A = 0, 512, 1024, 2048, 3072
ODD_PAD = 3200

FLASH_TILE = 256


def _cparams(sem):
    return pltpu.CompilerParams(dimension_semantics=sem, vmem_limit_bytes=VMEM_LIMIT)


def _rms(x, w):
    return x * lax.rsqrt(jnp.mean(x * x, axis=-1, keepdims=True) + EPS) * w


def _dot(a, b):
    return jnp.dot(a, b, preferred_element_type=F32)


def _dot_nt(a, b):
    return lax.dot_general(a, b, (((1,), (1,)), ((), ())), preferred_element_type=F32)


def _dot_tn(a, b):
    return lax.dot_general(a, b, (((0,), (0,)), ((), ())), preferred_element_type=F32)


def _sigmoid(x):
    return 1.0 / (1.0 + jnp.exp(-x))


def _norm_proj_kernel(x_ref, nw_ref, w_ref, o_ref, *, col_chunk):
    h = _rms(x_ref[...], nw_ref[...]).astype(BF16)
    n_cols = o_ref.shape[1]
    for c0 in range(0, n_cols, col_chunk):
        c1 = min(c0 + col_chunk, n_cols)
        o_ref[:, c0:c1] = _dot(h, w_ref[:, c0:c1])


def norm_proj(x, norm_w, w, tm):
    n, d = x.shape
    c = w.shape[1]
    return pl.pallas_call(
        functools.partial(_norm_proj_kernel, col_chunk=512),
        out_shape=jax.ShapeDtypeStruct((n, c), F32),
        grid=(n // tm,),
        in_specs=[pl.BlockSpec((tm, d), lambda i: (i, 0)),
                  pl.BlockSpec((1, d), lambda i: (0, 0)),
                  pl.BlockSpec((d, c), lambda i: (0, 0))],
        out_specs=pl.BlockSpec((tm, c), lambda i: (i, 0)),
        compiler_params=_cparams(("parallel",)),
        name="norm_proj",
    )(x, norm_w.reshape(1, d), w)


def _out_mlp_kernel(*refs, n_parts, final_norm):
    x_ref = refs[0]
    o_refs = refs[1:1 + n_parts]
    wo_ref, nw_ref, wup_ref, wdn_ref, nf_ref, y_ref, xm_sc, h_sc, acc_sc = refs[1 + n_parts:]
    j = pl.program_id(1)

    @pl.when(j == 0)
    def _():
        xm = x_ref[...]
        r0 = 0
        for o_ref in o_refs:
            w = o_ref.shape[1]
            xm = xm + _dot(o_ref[...].astype(BF16), wo_ref[r0:r0 + w, :])
            r0 += w
        xm_sc[...] = xm
        h_sc[...] = _rms(xm, nw_ref[...]).astype(BF16)
        acc_sc[...] = jnp.zeros_like(acc_sc)

    a = jnp.maximum(_dot(h_sc[...], wup_ref[...]), 0.0)
    acc_sc[...] += _dot((a * a).astype(BF16), wdn_ref[...])

    @pl.when(j == pl.num_programs(1) - 1)
    def _():
        y = xm_sc[...] + acc_sc[...]
        if final_norm:
            y = _rms(y, nf_ref[...])
        y_ref[...] = y


def out_mlp(x, o_parts, w_out, norm_w, w_up, w_down, norm_final, final_norm, tm, tf):
    n, d = x.shape
    dff = w_up.shape[1]
    n_parts = len(o_parts)
    in_specs = [pl.BlockSpec((tm, d), lambda i, j: (i, 0))]
    in_specs += [pl.BlockSpec((tm, o.shape[1]), lambda i, j: (i, 0)) for o in o_parts]
    in_specs += [pl.BlockSpec((d, d), lambda i, j: (0, 0)),
                 pl.BlockSpec((1, d), lambda i, j: (0, 0)),
                 pl.BlockSpec((d, tf), lambda i, j: (0, j)),
                 pl.BlockSpec((tf, d), lambda i, j: (j, 0)),
                 pl.BlockSpec((1, d), lambda i, j: (0, 0))]
    return pl.pallas_call(
        functools.partial(_out_mlp_kernel, n_parts=n_parts, final_norm=final_norm),
        out_shape=jax.ShapeDtypeStruct((n, d), F32),
        grid=(n // tm, dff // tf),
        in_specs=in_specs,
        out_specs=pl.BlockSpec((tm, d), lambda i, j: (i, 0)),
        scratch_shapes=[pltpu.VMEM((tm, d), F32), pltpu.VMEM((tm, d), BF16), pltpu.VMEM((tm, d), F32)],
        compiler_params=_cparams(("parallel", "arbitrary")),
        name="out_mlp",
    )(x, *o_parts, w_out, norm_w.reshape(1, d), w_up, w_down, norm_final.reshape(1, d))


def _pad_rows(a, rows):
    if a.shape[0] == rows:
        return a
    return jnp.concatenate([a, jnp.zeros((rows - a.shape[0], a.shape[1]), a.dtype)], axis=0)


def _chunk_core(q, k, v, g, st_ref, b_sc, k_sc):
    L, K = q.shape
    row = lax.broadcasted_iota(jnp.int32, (L, LANES), 0)
    col = lax.broadcasted_iota(jnp.int32, (L, LANES), 1)
    tri = jnp.where(col <= row, 1.0, 0.0).astype(BF16)
    g_hi = g.astype(BF16)
    r1 = g - g_hi.astype(F32)
    g_mid = r1.astype(BF16)
    g_lo = (r1 - g_mid.astype(F32)).astype(BF16)
    b = (_dot(tri, _pad_rows(g_hi, LANES)) + _dot(tri, _pad_rows(g_mid, LANES))
         + _dot(tri, _pad_rows(g_lo, LANES)))
    b_sc[...] = b
    k_sc[...] = k
    b_last = b[L - 1:L, :]
    st = st_ref[...]
    o = _dot_nt((q * jnp.exp(b)).astype(BF16), st.astype(BF16))

    lane = lax.broadcasted_iota(jnp.int32, (SUB, LANES), 1)
    srow = lax.broadcasted_iota(jnp.int32, (SUB, LANES), 0)
    blocks = []
    for i in range(L // SUB):
        qi = q[SUB * i:SUB * (i + 1), :]
        bi = b[SUB * i:SUB * (i + 1), :]
        a_i = jnp.zeros((SUB, LANES), F32)
        for sl in range(SUB):
            s = SUB * i + sl
            bs = b_sc[s:s + 1, :]
            ks = k_sc[s:s + 1, :]
            dec = jnp.exp(jnp.minimum(bi - bs, 0.0))
            c = jnp.sum(qi * dec * ks, axis=-1, keepdims=True)
            a_i = jnp.where(lane == s, c, a_i)
        if i > 0:
            r = b[SUB * i - 1:SUB * i, :]
            qq = qi * jnp.exp(jnp.minimum(bi - r, 0.0))
            kk = k * jnp.exp(jnp.minimum(r - b, 0.0))
            off = _dot_nt(qq.astype(BF16), _pad_rows(kk.astype(BF16), LANES))
            a_i = jnp.where(lane < SUB * i, off, a_i)
        a_i = jnp.where(lane <= srow + SUB * i, a_i, 0.0)
        blocks.append(a_i)
    a = blocks[0] if len(blocks) == 1 else jnp.concatenate(blocks, axis=0)
    v_pad = _pad_rows(v.astype(BF16), LANES)
    o = o + _dot(a.astype(BF16), v_pad)
    k_dec = k * jnp.exp(b_last - b)
    st_ref[...] = st * jnp.exp(b_last) + _dot_tn(v_pad, _pad_rows(k_dec.astype(BF16), LANES))
    return o


def _scan_body(gates_fn, post_fn, s0_ref, o_ref, sout_ref, st_sc, b_sc, k_sc, *, rows, L, H, V):
    t = pl.program_id(1)

    @pl.when(t == 0)
    def _():
        for h in range(H):
            st_sc[h] = s0_ref[h].T

    n_chunks = o_ref.shape[0] // rows

    def chunk(r0):
        for h in range(H):
            q, k, v, g, aux = gates_fn(r0, h)
            o = _chunk_core(_pad_rows(q, L), _pad_rows(k, L), _pad_rows(v, L), _pad_rows(g, L),
                            st_sc.at[h], b_sc.at[h], k_sc.at[h])
            o_ref[pl.ds(r0, rows), h * V:(h + 1) * V] = post_fn(o[:rows], aux).astype(o_ref.dtype)

    def loop_body(c, carry):
        chunk(pl.multiple_of(c * rows, rows))
        return carry

    if n_chunks == 1:
        chunk(0)
    else:
        lax.fori_loop(0, n_chunks, loop_body, 0)

    @pl.when(t == pl.num_programs(1) - 1)
    def _():
        for h in range(H):
            sout_ref[h] = st_sc[h].T


def _hgrn_kernel(q_ref, f_ref, i_ref, g_ref, lb_ref, nw_ref, s0_ref, o_ref, sout_ref,
                 st_sc, b_sc, k_sc, *, layer, rows, L):
    K, V, H = HG_KDIM, HG_VDIM, HG_HEADS
    lbr = [lb_ref[i:i + 1, :] for i in range(lb_ref.shape[0])]
    mx = functools.reduce(jnp.maximum, lbr)
    ex = [jnp.exp(r - mx) for r in lbr]
    den = functools.reduce(lambda a, c: a + c, ex)
    p = [e / den for e in ex]
    cum = p[0]
    for i in range(1, layer + 1):
        cum = cum + p[i]
    lb_all = cum - p[0]

    def gates(r0, h):
        hs = slice(h * K, (h + 1) * K)
        lb = lb_all[:, hs]
        z = f_ref[pl.ds(r0, rows), hs]
        qr = q_ref[pl.ds(r0, rows), hs]
        f = lb + (1.0 - lb) * _sigmoid(z)
        g = jnp.log(jnp.maximum(f, TINY))
        k = (1.0 - lb) * _sigmoid(-z)
        q = qr * _sigmoid(qr) * (K ** -0.5)
        vs = slice(h * V, (h + 1) * V)
        return q, k, i_ref[pl.ds(r0, rows), vs], g, g_ref[pl.ds(r0, rows), vs]

    def post(o, gr):
        return _rms(o, nw_ref[...]) * (gr * _sigmoid(gr))

    _scan_body(gates, post, s0_ref, o_ref, sout_ref, st_sc, b_sc, k_sc, rows=rows, L=L, H=H, V=V)


def _gla_kernel(q_ref, k_ref, v_ref, r_ref, a_ref, wa_ref, ba_ref, nw_ref, s0_ref, o_ref, sout_ref,
                st_sc, b_sc, k_sc, *, rows, L):
    K, V, H = GLA_HK, GLA_HV, GLA_HEADS

    def gates(r0, h):
        hs = slice(h * K, (h + 1) * K)
        vs = slice(h * V, (h + 1) * V)
        a1 = _pad_rows(a_ref[pl.ds(r0, rows), :], L).astype(BF16)
        x = (_dot(a1, wa_ref[:, hs]) + ba_ref[:, hs])[:rows]
        g = (jnp.minimum(x, 0.0) - jnp.log(1.0 + jnp.exp(-jnp.abs(x)))) / GATE_TAU
        q = q_ref[pl.ds(r0, rows), hs] * (K ** -0.5)
        return q, k_ref[pl.ds(r0, rows), hs], v_ref[pl.ds(r0, rows), vs], g, r_ref[pl.ds(r0, rows), vs]

    def post(o, rr):
        return _rms(o, nw_ref[...]) * (rr * _sigmoid(rr))

    _scan_body(gates, post, s0_ref, o_ref, sout_ref, st_sc, b_sc, k_sc, rows=rows, L=L, H=H, V=V)


def _scan_geometry(T):
    rows = min(CHUNK, T)
    L = max(rows, SUB)
    tc = min(T, 512)
    return tc, rows, L


def _scan_call(kern, name, ins, in_specs, s0_all, layer, B, T, H, K, V, out_dtype):
    tc, rows, L = _scan_geometry(T)
    nt = T // tc
    if s0_all is None:
        s0_all = jnp.zeros((1, B, H, K, V), F32)
        e_s0 = 0
    else:
        e_s0 = layer
    return pl.pallas_call(
        functools.partial(kern, rows=rows, L=L),
        out_shape=(jax.ShapeDtypeStruct((B * T, H * V), out_dtype),
                   jax.ShapeDtypeStruct((B, H, K, V), F32)),
        grid=(B, nt),
        in_specs=in_specs(tc, nt) + [
            pl.BlockSpec((None, None, H, K, V), lambda b, t: (e_s0, b, 0, 0, 0))],
        out_specs=(pl.BlockSpec((tc, H * V), lambda b, t: (b * nt + t, 0)),
                   pl.BlockSpec((None, H, K, V), lambda b, t: (b, 0, 0, 0))),
        scratch_shapes=[pltpu.VMEM((H, V, K), F32), pltpu.VMEM((H, L, K), F32),
                        pltpu.VMEM((H, L, K), F32)],
        compiler_params=_cparams(("parallel", "arbitrary")),
        name=name,
    )(*ins, s0_all)


def hgrn_scan(P, lower_bounds, norm_w, s0_all, layer, B, T, out_dtype):
    K, V, H = HG_KDIM, HG_VDIM, HG_HEADS
    W = H * K

    def in_specs(tc, nt):
        def colspec(off):
            return pl.BlockSpec((tc, W), lambda b, t: (b * nt + t, off // W))
        return [colspec(E_Q), colspec(E_F), colspec(E_I), colspec(E_G),
                pl.BlockSpec(lower_bounds.shape, lambda b, t: (0, 0)),
                pl.BlockSpec((1, V), lambda b, t: (0, 0))]

    return _scan_call(functools.partial(_hgrn_kernel, layer=layer), "hgrn_scan",
                      (P, P, P, P, lower_bounds, norm_w.reshape(1, V)), in_specs,
                      s0_all, layer, B, T, H, K, V, out_dtype)


def gla_scan(P, w_a2p, b_a, norm_w, s0_all, layer, B, T, out_dtype):
    K, V, H = GLA_HK, GLA_HV, GLA_HEADS

    def in_specs(tc, nt):
        def colspec(off, w):
            return pl.BlockSpec((tc, w), lambda b, t: (b * nt + t, off // w))
        return [colspec(O_Q, GLA_KEY), colspec(O_K, GLA_KEY), colspec(O_V, GLA_VAL), colspec(O_R, GLA_VAL),
                colspec(O_A, LANES),
                pl.BlockSpec((LANES, GLA_KEY), lambda b, t: (0, 0)),
                pl.BlockSpec((1, GLA_KEY), lambda b, t: (0, 0)),
                pl.BlockSpec((1, V), lambda b, t: (0, 0))]

    return _scan_call(_gla_kernel, "gla_scan",
                      (P, P, P, P, P, w_a2p, b_a.reshape(1, GLA_KEY), norm_w.reshape(1, V)), in_specs,
                      s0_all, layer, B, T, H, K, V, out_dtype)


def _mla_prep_kernel(cq_ref, kr_ref, ckv_ref, qnw_ref, kvnw_ref, wn_ref, wpe_ref, wper_ref, wuk_ref,
                     ta_ref, tb_ref, tc_ref, rows_ref, kvp_ref, qc_ref, *maybe_vt_ref, transposed):
    cqn = _rms(cq_ref[...], qnw_ref[...]).astype(BF16)
    qn = _dot(cqn, wn_ref[...])
    qpe = _dot(cqn, wpe_ref[...])
    qper = _dot(cqn, wper_ref[...])
    ta = ta_ref[...]
    tb = tb_ref[...]
    for h in range(MLA_HEADS):
        sl = slice(h * LANES, (h + 1) * LANES)
        q_lat = _dot(qn[:, sl].astype(BF16), wuk_ref[h])
        q_pe = qpe[:, sl] * ta + qper[:, sl] * tb
        if transposed:
            qh = jnp.concatenate([q_lat, q_pe], axis=1)
            qc_ref[h * LAT_PAD:(h + 1) * LAT_PAD, :] = qh.T.astype(qc_ref.dtype)
        else:
            qc_ref[:, h * LAT_PAD:h * LAT_PAD + KV_LORA] = q_lat.astype(qc_ref.dtype)
            qc_ref[:, h * LAT_PAD + KV_LORA:(h + 1) * LAT_PAD] = q_pe.astype(qc_ref.dtype)
    ckvn = _rms(ckv_ref[...], kvnw_ref[...])
    pk = kr_ref[...] * tc_ref[...]
    kf = pk + pltpu.roll(pk, QK_ROPE, axis=1)
    lane = lax.broadcasted_iota(jnp.int32, kf.shape, 1)
    kpe = jnp.where(lane < QK_ROPE, kf, 0.0)
    rows_ref[:, :KV_LORA] = ckvn
    rows_ref[:, KV_LORA:] = kf[:, :QK_ROPE]
    kvp_ref[:, :KV_LORA] = ckvn.astype(kvp_ref.dtype)
    kvp_ref[:, KV_LORA:] = kpe.astype(kvp_ref.dtype)
    if transposed:
        maybe_vt_ref[0][...] = ckvn.T.astype(BF16)


def mla_prep(P, qn_w, kvn_w, wn, wpe, wper, wukT, tabs, tab_blocks, tm, mid_dtype, transposed):
    n = P.shape[0]
    nb = n // tm
    ta, tb, tcc = tabs
    QW = MLA_HEADS * LAT_PAD

    def tabspec():
        return pl.BlockSpec((tm, LANES), lambda i: (i % tab_blocks, 0))

    def full(a):
        nd = a.ndim
        return pl.BlockSpec(a.shape, lambda i: (0,) * nd)

    qn_w = qn_w.reshape(1, Q_LORA)
    kvn_w = kvn_w.reshape(1, KV_LORA)
    out_shape = [jax.ShapeDtypeStruct((n, LAT_DIM), F32), jax.ShapeDtypeStruct((n, LAT_PAD), mid_dtype)]
    out_specs = [pl.BlockSpec((tm, LAT_DIM), lambda i: (i, 0)), pl.BlockSpec((tm, LAT_PAD), lambda i: (i, 0))]
    if transposed:
        out_shape += [jax.ShapeDtypeStruct((nb, QW, tm), mid_dtype),
                      jax.ShapeDtypeStruct((nb, KV_LORA, tm), BF16)]
        out_specs += [pl.BlockSpec((None, QW, tm), lambda i: (i, 0, 0)),
                      pl.BlockSpec((None, KV_LORA, tm), lambda i: (i, 0, 0))]
    else:
        out_shape += [jax.ShapeDtypeStruct((n, QW), mid_dtype)]
        out_specs += [pl.BlockSpec((tm, QW), lambda i: (i, 0))]
    return pl.pallas_call(
        functools.partial(_mla_prep_kernel, transposed=transposed),
        out_shape=tuple(out_shape),
        grid=(nb,),
        in_specs=[pl.BlockSpec((tm, Q_LORA), lambda i: (i, E_CQ // Q_LORA)),
                  pl.BlockSpec((tm, LANES), lambda i: (i, E_KR // LANES)),
                  pl.BlockSpec((tm, KV_LORA), lambda i: (i, E_CKV // KV_LORA)),
                  full(qn_w), full(kvn_w), full(wn), full(wpe), full(wper), full(wukT),
                  tabspec(), tabspec(), tabspec()],
        out_specs=tuple(out_specs),
        compiler_params=_cparams(("parallel",)),
        name="mla_prep",
    )(P, P, P, qn_w, kvn_w, wn, wpe, wper, wukT, ta, tb, tcc)


def _flash_kernel(q_ref, kv_ref, vt_ref, wuvt_ref, o_ref, qs_sc, m_sc, l_sc, acc_sc, *, tq):
    qi = pl.program_id(1)
    H = MLA_HEADS
    c = MLA_SCALE * LOG2E
    for h in range(H):
        qs_sc[:, h * tq:(h + 1) * tq] = q_ref[h * LAT_PAD:(h + 1) * LAT_PAD, :]
    m_sc[...] = jnp.full_like(m_sc, NEG_BIG)
    l_sc[...] = jnp.zeros_like(l_sc)
    acc_sc[...] = jnp.zeros_like(acc_sc)

    def block(kj, masked):
        k0 = pl.multiple_of(kj * tq, tq)
        s = _dot(kv_ref[pl.ds(k0, tq), :], qs_sc[...])
        if masked:
            key = lax.broadcasted_iota(jnp.int32, s.shape, 0)
            tok = lax.broadcasted_iota(jnp.int32, s.shape, 1) & (tq - 1)
            s = jnp.where(key <= tok, s, NEG_BIG)
        m_prev = m_sc[...]
        m_new = jnp.maximum(m_prev, jnp.max(s, axis=0, keepdims=True))
        alpha = jnp.exp2((m_prev - m_new) * c)
        p = jnp.exp2(s * c - m_new * c)
        l_sc[...] = alpha * l_sc[...] + jnp.sum(p, axis=0, keepdims=True)
        acc_sc[...] = alpha * acc_sc[...] + _dot(vt_ref[kj], p.astype(BF16))
        m_sc[...] = m_new

    def body(kj, carry):
        block(kj, False)
        return carry

    lax.fori_loop(0, qi, body, 0)
    block(qi, True)
    out_t = (acc_sc[...] * (1.0 / l_sc[...])).astype(BF16)
    for h in range(H):
        oh_t = _dot(wuvt_ref[h], out_t[:, h * tq:(h + 1) * tq])
        o_ref[:, h * V_HEAD:(h + 1) * V_HEAD] = oh_t.T.astype(o_ref.dtype)


def flash_prompt(qct, kvp, vt, wuvt, B, T):
    tq = FLASH_TILE
    nq = T // tq
    H = MLA_HEADS
    return pl.pallas_call(
        functools.partial(_flash_kernel, tq=tq),
        out_shape=jax.ShapeDtypeStruct((B * T, H * V_HEAD), BF16),
        grid=(B, nq),
        in_specs=[pl.BlockSpec((None, H * LAT_PAD, tq), lambda b, i: (b * nq + i, 0, 0)),
                  pl.BlockSpec((T, LAT_PAD), lambda b, i: (b, 0)),
                  pl.BlockSpec((nq, KV_LORA, tq), lambda b, i: (b, 0, 0)),
                  pl.BlockSpec(wuvt.shape, lambda b, i: (0, 0, 0))],
        out_specs=pl.BlockSpec((tq, H * V_HEAD), lambda b, i: (b * nq + i, 0)),
        scratch_shapes=[pltpu.VMEM((LAT_PAD, H * tq), BF16), pltpu.VMEM((1, H * tq), F32),
                        pltpu.VMEM((1, H * tq), F32), pltpu.VMEM((KV_LORA, H * tq), F32)],
        compiler_params=_cparams(("parallel", "arbitrary")),
        name="flash_prompt",
    )(qct, kvp, vt, wuvt)


def _softmax_part(s, v_nt, c):
    m = jnp.max(s, axis=-1, keepdims=True)
    p = jnp.exp2((s - m) * c)
    return m, jnp.sum(p, axis=-1, keepdims=True), _dot_nt(p.astype(BF16), v_nt)


def _merge_parts(parts, m_sc, l_sc, acc_sc, c):
    m_prev = m_sc[...]
    m_new = m_prev
    for m_g, _, _ in parts:
        m_new = jnp.maximum(m_new, m_g)
    alpha = jnp.exp2((m_prev - m_new) * c)
    l = alpha * l_sc[...]
    acc = alpha * acc_sc[...]
    for m_g, l_g, a_g in parts:
        w = jnp.exp2((m_g - m_new) * c)
        l = l + w * l_g
        acc = acc + w * a_g
    m_sc[...] = m_new
    l_sc[...] = l
    acc_sc[...] = acc


def _paged_kernel(pt_ref, *refs, n_pages, group, td):
    page_refs = refs[:n_pages]
    q_ref, new_ref, wuv_ref, o_ref, qs_sc, m_sc, l_sc, acc_sc = refs[n_pages:]
    step = pl.program_id(1)
    H = MLA_HEADS
    c = MLA_SCALE * LOG2E

    @pl.when(step == 0)
    def _():
        for h in range(H):
            qs_sc[h * td:(h + 1) * td, :] = q_ref[:, h * LAT_PAD:(h + 1) * LAT_PAD]
        m_sc[...] = jnp.full_like(m_sc, NEG_BIG)
        l_sc[...] = jnp.zeros_like(l_sc)
        acc_sc[...] = jnp.zeros_like(acc_sc)

    qs = qs_sc[...].astype(BF16)
    parts = []
    for g0 in range(0, n_pages, group):
        kt = jnp.concatenate([page_refs[j][...] for j in range(g0, g0 + group)], axis=1).astype(BF16)
        s = _dot(qs, _pad_rows(kt, LAT_PAD))
        parts.append(_softmax_part(s, kt[:KV_LORA, :], c))
    _merge_parts(parts, m_sc, l_sc, acc_sc, c)

    @pl.when(step == pl.num_programs(1) - 1)
    def _():
        kn = _pad_rows(new_ref[...], LANES).astype(BF16)
        sn = _dot_nt(qs, kn)
        row = lax.broadcasted_iota(jnp.int32, sn.shape, 0) & (td - 1)
        col = lax.broadcasted_iota(jnp.int32, sn.shape, 1)
        sn = jnp.where(col <= row, sn, NEG_BIG)
        m = jnp.max(sn, axis=-1, keepdims=True)
        p = jnp.exp2((sn - m) * c)
        part = (m, jnp.sum(p, axis=-1, keepdims=True), _dot(p.astype(BF16), kn[:, :KV_LORA]))
        _merge_parts([part], m_sc, l_sc, acc_sc, c)
        out = (acc_sc[...] / l_sc[...]).astype(BF16)
        for h in range(H):
            o_ref[:, h * V_HEAD:(h + 1) * V_HEAD] = _dot(out, wuv_ref[h])[h * td:(h + 1) * td, :]


def paged_sample(page_table, cache_t, layer, qc, kvp, wuv, Bd, Td, pages_per_step, group):
    n_log = page_table.shape[1]
    H = MLA_HEADS
    npp = pages_per_step
    pt = page_table.reshape(-1)

    def page_spec(j):
        return pl.BlockSpec((None, None, LAT_DIM, PAGE_SIZE),
                            lambda b, c, pt_ref: (layer, pt_ref[b * n_log + c * npp + j], 0, 0))

    grid_spec = pltpu.PrefetchScalarGridSpec(
        num_scalar_prefetch=1,
        grid=(Bd, n_log // npp),
        in_specs=[page_spec(j) for j in range(npp)] + [
            pl.BlockSpec((Td, H * LAT_PAD), lambda b, c, pt_ref: (b, 0)),
            pl.BlockSpec((Td, LAT_PAD), lambda b, c, pt_ref: (b, 0)),
            pl.BlockSpec(wuv.shape, lambda b, c, pt_ref: (0, 0, 0))],
        out_specs=pl.BlockSpec((Td, H * V_HEAD), lambda b, c, pt_ref: (b, 0)),
        scratch_shapes=[pltpu.VMEM((H * Td, LAT_PAD), F32),
                        pltpu.VMEM((H * Td, 1), F32), pltpu.VMEM((H * Td, 1), F32),
                        pltpu.VMEM((H * Td, KV_LORA), F32)])
    return pl.pallas_call(
        functools.partial(_paged_kernel, n_pages=npp, group=group, td=Td),
        out_shape=jax.ShapeDtypeStruct((Bd * Td, H * V_HEAD), F32),
        grid_spec=grid_spec,
        compiler_params=_cparams(("parallel", "arbitrary")),
        name="paged_sample",
    )(pt, *([cache_t] * npp), qc, kvp, wuv)


def _rope_tables(pos):
    half = QK_ROPE // 2
    inv = ROPE_THETA ** (-jnp.arange(half, dtype=F32) / half)
    ang = pos.astype(F32)[:, None] * inv[None, :]
    cos2 = jnp.concatenate([jnp.cos(ang)] * 2, axis=1)
    sin_s = jnp.concatenate([-jnp.sin(ang), jnp.sin(ang)], axis=1)
    return (jnp.concatenate([cos2, cos2], axis=1), jnp.concatenate([sin_s, sin_s], axis=1),
            jnp.concatenate([cos2, sin_s], axis=1))


def _swap_halves(w):
    half = QK_ROPE // 2
    shp = w.shape
    w = w.reshape(shp[:-1] + (shp[-1] // QK_ROPE, 2, half))
    return w[..., ::-1, :].reshape(shp)


def _even_weights(w_in, w_uq, w_uk, w_uv):
    hq, hf, hi, hg, cq, ckv, kr = jnp.split(
        w_in, np.cumsum([HG_QDIM, HG_QDIM, HG_WIDTH, HG_WIDTH, Q_LORA, KV_LORA]).tolist(), axis=1)
    w_in_p = jnp.concatenate([cq, kr, _swap_halves(kr), hq, hf, hi, hg, ckv], axis=1).astype(BF16)
    wq = w_uq.reshape(Q_LORA, MLA_HEADS, QK_NOPE + QK_ROPE)
    wn = wq[:, :, :QK_NOPE].reshape(Q_LORA, MLA_HEADS * QK_NOPE).astype(BF16)
    wpe = wq[:, :, QK_NOPE:]
    zpad = jnp.zeros_like(wpe)
    wpe_p = jnp.concatenate([wpe, zpad], axis=2).reshape(Q_LORA, MLA_HEADS * LANES).astype(BF16)
    wper_p = jnp.concatenate([_swap_halves(wpe), zpad], axis=2).reshape(Q_LORA, MLA_HEADS * LANES).astype(BF16)
    wukT = jnp.swapaxes(w_uk, 1, 2).astype(BF16)
    wuv = w_uv.astype(BF16)
    return w_in_p, wn, wpe_p, wper_p, wukT, wuv, jnp.swapaxes(wuv, 1, 2)


def _odd_weights(w_in, w_a2):
    w_in_p = jnp.pad(w_in, ((0, 0), (0, ODD_PAD - w_in.shape[1]))).astype(BF16)
    w_a2p = jnp.pad(w_a2, ((0, LANES - GATE_RANK), (0, 0))).astype(BF16)
    return w_in_p, w_a2p


def kernel(x_prompt, x_sample, cache_mla, state_hgrn, state_gla, page_table, norm_mix, norm_mlp, norm_final, w_in_even, lower_bounds, hgrn_norm, q_norm, w_uq, kv_norm, w_uk, w_uv, w_out_even, w_in_odd, w_alpha2, b_alpha, gla_norm, w_out_odd, w_up, w_down):
    B, T, D = x_prompt.shape
    Bd, Td, _ = x_sample.shape
    past_len = page_table.shape[1] * PAGE_SIZE
    xp = x_prompt.reshape(B * T, D)
    xs = x_sample.reshape(Bd * Td, D)
    tm_p, tm_s = 1024, Bd * Td
    tabs_p = _rope_tables(jnp.arange(T))
    tabs_s = tuple(jnp.tile(t, (Bd, 1)) for t in _rope_tables(past_len + jnp.arange(Td)))
    w_up_b = w_up.astype(BF16)
    w_down_b = w_down.astype(BF16)
    cache_t = jnp.swapaxes(cache_mla, 2, 3)

    mla_p, mla_s, hg_p, hg_s, gla_p, gla_s = [], [], [], [], [], []
    for layer in range(DEPTH):
        last = layer == DEPTH - 1
        if layer % 2 == 0:
            e = layer // 2
            w_in_p, wn, wpe_p, wper_p, wukT, wuv, wuvT = _even_weights(w_in_even[e], w_uq[e], w_uk[e], w_uv[e])
            w_out = w_out_even[e].astype(BF16)
            Pp = norm_proj(xp, norm_mix[layer], w_in_p, tm_p)
            Ps = norm_proj(xs, norm_mix[layer], w_in_p, tm_s)
            oa_p, sp = hgrn_scan(Pp, lower_bounds, hgrn_norm[e], None, e, B, T, BF16)
            oa_s, ss = hgrn_scan(Ps, lower_bounds, hgrn_norm[e], state_hgrn, e, Bd, Td, F32)
            rows_p, kvp_p, qct_p, vt_p = mla_prep(Pp, q_norm[e], kv_norm[e], wn, wpe_p, wper_p, wukT,
                                                  tabs_p, T // FLASH_TILE, FLASH_TILE, BF16, True)
            rows_s, kvp_s, qc_s = mla_prep(Ps, q_norm[e], kv_norm[e], wn, wpe_p, wper_p, wukT,
                                           tabs_s, 1, Bd * Td, F32, False)
            ob_p = flash_prompt(qct_p, kvp_p, vt_p, wuvT, B, T)
            ob_s = paged_sample(page_table, cache_t, e, qc_s, kvp_s, wuv, Bd, Td, 16, 4)
            mla_p.append(rows_p.reshape(B, T, LAT_DIM))
            mla_s.append(rows_s.reshape(Bd, Td, LAT_DIM))
            hg_p.append(sp)
            hg_s.append(ss)
            parts_p, parts_s = [oa_p, ob_p], [oa_s, ob_s]
        else:
            o = layer // 2
            w_in_p, w_a2p = _odd_weights(w_in_odd[o], w_alpha2[o])
            w_out = w_out_odd[o].astype(BF16)
            Pp = norm_proj(xp, norm_mix[layer], w_in_p, tm_p)
            Ps = norm_proj(xs, norm_mix[layer], w_in_p, tm_s)
            o_p, sp = gla_scan(Pp, w_a2p, b_alpha[o], gla_norm[o], None, o, B, T, BF16)
            o_s, ss = gla_scan(Ps, w_a2p, b_alpha[o], gla_norm[o], state_gla, o, Bd, Td, F32)
            gla_p.append(sp)
            gla_s.append(ss)
            parts_p, parts_s = [o_p], [o_s]
        xp = out_mlp(xp, parts_p, w_out, norm_mlp[layer], w_up_b[layer], w_down_b[layer],
                     norm_final, last, tm_p, 1024)
        xs = out_mlp(xs, parts_s, w_out, norm_mlp[layer], w_up_b[layer], w_down_b[layer],
                     norm_final, last, tm_s, 1024)

    return (xp.reshape(B, T, D), xs.reshape(Bd, Td, D), jnp.stack(mla_p), jnp.stack(mla_s),
            jnp.stack(hg_p), jnp.stack(hg_s), jnp.stack(gla_p), jnp.stack(gla_s))
```
